```python
import jax, jax.numpy as jnp
from jax import lax
import numpy as np

D_MODEL = 2048
BATCH = 4
SEQ = 4096
DEPTH = 2

N_RET_HEADS = 8
RET_HEAD_DIM = 128
RET_WIDTH = N_RET_HEADS * RET_HEAD_DIM
N_MLA_HEADS = 8
MLA_NOPE_DIM = 128
MLA_ROPE_DIM = 64
MLA_QK_DIM = MLA_NOPE_DIM + MLA_ROPE_DIM
MLA_V_DIM = 128
MLA_Q_RANK = 512
MLA_KV_RANK = 512
MLA_WIDTH = N_MLA_HEADS * MLA_V_DIM
MIX_WIDTH_A = RET_WIDTH + MLA_WIDTH
SPLIT_A = (RET_WIDTH, 2 * RET_WIDTH, 3 * RET_WIDTH,
           3 * RET_WIDTH + MLA_Q_RANK,
           3 * RET_WIDTH + MLA_Q_RANK + MLA_KV_RANK,
           3 * RET_WIDTH + MLA_Q_RANK + MLA_KV_RANK + MLA_ROPE_DIM)
IN_WIDTH_A = SPLIT_A[-1] + MIX_WIDTH_A
CONV_WIDTH = D_MODEL
CONV_KERNEL = 31
IN_WIDTH_C = 3 * CONV_WIDTH

CHUNK = 128
Q_BLOCK = 128
ROPE_BASE = 10000.0
EPS = 1e-6
N_EVEN = (DEPTH + 1) // 2
N_ODD = DEPTH // 2

kernel_name = 'hybrid_retention_mla_conformer'


def rms_norm(x, g):
    xf = x.astype(jnp.float32)
    y = xf * lax.rsqrt(jnp.mean(xf * xf, axis=-1, keepdims=True) + EPS)
    return (y * g.astype(jnp.float32)).astype(x.dtype)


def layer_norm(x, g, b):
    xf = x.astype(jnp.float32)
    mu = jnp.mean(xf, axis=-1, keepdims=True)
    xc = xf - mu
    y = xc * lax.rsqrt(jnp.mean(xc * xc, axis=-1, keepdims=True) + EPS)
    return (y * g.astype(jnp.float32) + b.astype(jnp.float32)).astype(x.dtype)


def rope(x, pos):
    d = x.shape[-1]
    inv = ROPE_BASE ** (-jnp.arange(0, d, 2, dtype=jnp.float32) / d)
    ang = pos.astype(jnp.float32)[..., None] * inv
    cos = jnp.cos(ang)[:, :, None, :]
    sin = jnp.sin(ang)[:, :, None, :]
    xf = x.astype(jnp.float32)
    x1, x2 = xf[..., : d // 2], xf[..., d // 2:]
    out = jnp.concatenate([x1 * cos - x2 * sin, x2 * cos + x1 * sin], axis=-1)
    return out.astype(x.dtype)


def retention_chunkwise(q, k, v):
    f32 = jnp.float32
    B, S, H, dk = q.shape
    dv = v.shape[-1]
    nc = S // CHUNK
    log_g = jnp.log1p(-(2.0 ** (-5.0 - jnp.arange(H, dtype=f32))))
    qc = q.astype(f32).reshape(B, nc, CHUNK, H, dk)
    kc = k.astype(f32).reshape(B, nc, CHUNK, H, dk) * (dk ** -0.5)
    vc = v.astype(f32).reshape(B, nc, CHUNK, H, dv)
    idx = jnp.arange(CHUNK, dtype=f32)
    rel = idx[:, None] - idx[None, :]
    causal = rel >= 0
    dmask = jnp.where(causal[None], jnp.exp(log_g[:, None, None] * jnp.where(causal, rel, 0.0)[None]), 0.0)
    scores = jnp.einsum('bnihd,bnjhd->bnhij', qc, kc) * dmask
    inner = jnp.einsum('bnhij,bnjhe->bnihe', scores, vc)
    to_end = jnp.exp(log_g[None, :] * (CHUNK - 1.0 - idx)[:, None])
    kv = jnp.einsum('bnjhd,jh,bnjhe->bnhde', kc, to_end, vc)
    chunk_decay = jnp.exp(log_g * CHUNK)[None, :, None, None]

    def step(state, kv_n):
        return chunk_decay * state + kv_n, state

    _, prev = lax.scan(step, jnp.zeros((B, H, dk, dv), f32), jnp.moveaxis(kv, 1, 0))
    prev = jnp.moveaxis(prev, 0, 1)
    from_start = jnp.exp(log_g[None, :] * (idx + 1.0)[:, None])
    cross = jnp.einsum('bnihd,ih,bnhde->bnihe', qc, from_start, prev)
    return (inner + cross).reshape(B, S, H, dv)


def causal_attention_blocked(q, k, v):
    f32 = jnp.float32
    B, S, H, d = q.shape
    dv = v.shape[-1]
    nb = S // Q_BLOCK
    scale = d ** -0.5
    kf = k.astype(f32)
    vf = v.astype(f32)
    qb = jnp.moveaxis(q.astype(f32).reshape(B, nb, Q_BLOCK, H, d), 1, 0)
    key_pos = jnp.arange(S)
    neg = jnp.finfo(f32).min

    def one_block(args):
        q_blk, bi = args
        s = jnp.einsum('bqhd,bkhd->bhqk', q_blk, kf) * scale
        q_pos = bi * Q_BLOCK + jnp.arange(Q_BLOCK)
        s = jnp.where((key_pos[None, :] <= q_pos[:, None])[None, None], s, neg)
        p = jax.nn.softmax(s, axis=-1)
        return jnp.einsum('bhqk,bkhe->bqhe', p, vf)

    out = lax.map(one_block, (qb, jnp.arange(nb)))
    return jnp.moveaxis(out, 0, 1).reshape(B, S, H, dv)


def layer_retention_mla(x, pos, norm_g, w_in, q_a_norm_g, w_q_b, kv_a_norm_g, w_kv_b,
                        q_norm_g, k_norm_g, ret_norm_g, w_out):
    B, S, _ = x.shape
    h = rms_norm(x, norm_g)
    proj = h @ w_in
    rq, rk, rv, cq, ckv, krope, gate = jnp.split(proj, list(SPLIT_A), axis=-1)
    rq = rope(rq.reshape(B, S, N_RET_HEADS, RET_HEAD_DIM), pos)
    rk = rope(rk.reshape(B, S, N_RET_HEADS, RET_HEAD_DIM), pos)
    rv = rv.reshape(B, S, N_RET_HEADS, RET_HEAD_DIM)
    ret = retention_chunkwise(rq, rk, rv)
    ret = rms_norm(ret, ret_norm_g.reshape(N_RET_HEADS, RET_HEAD_DIM)).reshape(B, S, RET_WIDTH)
    q = (rms_norm(cq, q_a_norm_g) @ w_q_b).reshape(B, S, N_MLA_HEADS, MLA_QK_DIM)
    kv = (rms_norm(ckv, kv_a_norm_g) @ w_kv_b).reshape(B, S, N_MLA_HEADS, MLA_NOPE_DIM + MLA_V_DIM)
    k_nope, v = kv[..., :MLA_NOPE_DIM], kv[..., MLA_NOPE_DIM:]
    k_rope = jnp.broadcast_to(krope[:, :, None, :], (B, S, N_MLA_HEADS, MLA_ROPE_DIM))
    k = jnp.concatenate([k_nope, k_rope], axis=-1)
    q = rms_norm(q, q_norm_g)
    k = rms_norm(k, k_norm_g)
    q = jnp.concatenate([q[..., :MLA_NOPE_DIM], rope(q[..., MLA_NOPE_DIM:], pos)], axis=-1)
    k = jnp.concatenate([k[..., :MLA_NOPE_DIM], rope(k[..., MLA_NOPE_DIM:], pos)], axis=-1)
    att = causal_attention_blocked(q, k, v).reshape(B, S, MLA_WIDTH)
    mix = jnp.concatenate([ret, att], axis=-1).astype(x.dtype) * jax.nn.silu(gate)
    return x + mix @ w_out


def layer_conformer_conv(x, norm_g, w_in, conv_w, conv_b, ln_g, ln_b, w_out):
    h = rms_norm(x, norm_g)
    a, b, gate = jnp.split(h @ w_in, 3, axis=-1)
    u = a * jax.nn.sigmoid(b)
    u = lax.conv_general_dilated(
        u, conv_w[:, None, :].astype(u.dtype), window_strides=(1,),
        padding=[(CONV_KERNEL - 1, 0)], dimension_numbers=('NWC', 'WIO', 'NWC'),
        feature_group_count=CONV_WIDTH) + conv_b
    u = jax.nn.silu(layer_norm(u, ln_g, ln_b))
    return x + (u * jax.nn.silu(gate)) @ w_out


def setup_inputs(seed: int = 0) -> dict:
    key = jax.random.key(seed)
    ks = jax.random.split(key, 24)
    f32 = jnp.float32

    def nrm(k, shape, scale):
        return jax.random.normal(k, shape, f32) * scale

    def gain(k, shape):
        return 1.0 + 0.02 * jax.random.normal(k, shape, f32)

    x = jax.random.normal(ks[0], (BATCH, SEQ, D_MODEL), f32)
    offs = jax.random.randint(ks[1], (BATCH, 1), 0, 1024, dtype=jnp.int32)
    positions = offs + jnp.arange(SEQ, dtype=jnp.int32)[None, :]
    E, O = N_EVEN, N_ODD
    return {
        'x': x,
        'positions': positions,
        'a_norm_g': gain(ks[2], (E, D_MODEL)),
        'a_w_in': nrm(ks[3], (E, D_MODEL, IN_WIDTH_A), D_MODEL ** -0.5),
        'a_q_a_norm_g': gain(ks[4], (E, MLA_Q_RANK)),
        'a_w_q_b': nrm(ks[5], (E, MLA_Q_RANK, N_MLA_HEADS * MLA_QK_DIM), MLA_Q_RANK ** -0.5),
        'a_kv_a_norm_g': gain(ks[6], (E, MLA_KV_RANK)),
        'a_w_kv_b': nrm(ks[7], (E, MLA_KV_RANK, N_MLA_HEADS * (MLA_NOPE_DIM + MLA_V_DIM)), MLA_KV_RANK ** -0.5),
        'a_q_norm_g': gain(ks[8], (E, MLA_QK_DIM)),
        'a_k_norm_g': gain(ks[9], (E, MLA_QK_DIM)),
        'a_ret_norm_g': gain(ks[10], (E, RET_WIDTH)),
        'a_w_out': nrm(ks[11], (E, MIX_WIDTH_A, D_MODEL), MIX_WIDTH_A ** -0.5),
        'c_norm_g': gain(ks[12], (O, D_MODEL)),
        'c_w_in': nrm(ks[13], (O, D_MODEL, IN_WIDTH_C), D_MODEL ** -0.5),
        'c_conv_w': nrm(ks[14], (O, CONV_KERNEL, CONV_WIDTH), CONV_KERNEL ** -0.5),
        'c_conv_b': 0.02 * jax.random.normal(ks[15], (O, CONV_WIDTH), f32),
        'c_ln_g': gain(ks[16], (O, CONV_WIDTH)),
        'c_ln_b': 0.02 * jax.random.normal(ks[17], (O, CONV_WIDTH), f32),
        'c_w_out': nrm(ks[18], (O, CONV_WIDTH, D_MODEL), CONV_WIDTH ** -0.5),
    }


def reference(x, positions, a_norm_g, a_w_in, a_q_a_norm_g, a_w_q_b, a_kv_a_norm_g, a_w_kv_b,
              a_q_norm_g, a_k_norm_g, a_ret_norm_g, a_w_out,
              c_norm_g, c_w_in, c_conv_w, c_conv_b, c_ln_g, c_ln_b, c_w_out):
    for layer in range(DEPTH):
        i = layer // 2
        if layer % 2 == 0:
            x = layer_retention_mla(x, positions, a_norm_g[i], a_w_in[i], a_q_a_norm_g[i], a_w_q_b[i],
                                    a_kv_a_norm_g[i], a_w_kv_b[i], a_q_norm_g[i], a_k_norm_g[i],
                                    a_ret_norm_g[i], a_w_out[i])
        else:
            x = layer_conformer_conv(x, c_norm_g[i], c_w_in[i], c_conv_w[i], c_conv_b[i],
                                     c_ln_g[i], c_ln_b[i], c_w_out[i])
    return x
```

```python
import functools
import math

import jax
import jax.numpy as jnp
from jax import lax
from jax.experimental import pallas as pl
from jax.experimental.pallas import tpu as pltpu

F32 = jnp.float32
BF16 = jnp.bfloat16

D_MODEL = 2048
N_RET_HEADS = 8
RET_HEAD_DIM = 128
RET_WIDTH = N_RET_HEADS * RET_HEAD_DIM
N_MLA_HEADS = 8
MLA_NOPE_DIM = 128
MLA_ROPE_DIM = 64
MLA_QK_DIM = MLA_NOPE_DIM + MLA_ROPE_DIM
MLA_V_DIM = 128
MLA_Q_RANK = 512
MLA_KV_RANK = 512
MLA_WIDTH = N_MLA_HEADS * MLA_V_DIM
MIX_WIDTH = RET_WIDTH + MLA_WIDTH
CONV_WIDTH = D_MODEL
CONV_KERNEL = 31
ROPE_BASE = 10000.0
EPS = 1e-6

LANES = 128
SUBLANES = 8
MLA_HEAD_PAD = 2 * LANES
VMEM_LIMIT_BYTES = 56 * 1024 * 1024

COL_GATE = 0
COL_RQ = MIX_WIDTH
COL_RK = COL_RQ + RET_WIDTH
COL_RV = COL_RK + RET_WIDTH
COL_CQ = COL_RV + RET_WIDTH
COL_CKV = COL_CQ + MLA_Q_RANK
COL_KROPE = COL_CKV + MLA_KV_RANK
IN_WIDTH_A_PAD = COL_KROPE + LANES

RET_CHUNK = 256
RET_LOG_G = tuple(math.log1p(-(2.0 ** (-5.0 - h))) for h in range(N_RET_HEADS))

ATT_TQ = 512
ATT_TK = 512
NEG_BIG = -1e30

CONV_ROWS = 64
CONV_HALO = 32


def _params(sem):
    return pltpu.CompilerParams(dimension_semantics=sem, vmem_limit_bytes=VMEM_LIMIT_BYTES)


def _rope_tables_kernel(pos_ref, c_ref, cos_a_ref, sin_a_ref, cos_b_ref, sin_b_ref):
    ang = pos_ref[...].astype(F32) * c_ref[0:1, :]
    c = jnp.cos(ang)
    s = jnp.sin(ang)
    cr = pltpu.roll(c, 64, 1)
    sr = pltpu.roll(s, 64, 1)
    lane = lax.broadcasted_iota(jnp.int32, c.shape, 1)
    low = lane < 64
    cos_a_ref[...] = jnp.where(low, c, cr)
    sin_a_ref[...] = jnp.where(low, s, sr) * c_ref[1:2, :]
    cos_b_ref[...] = jnp.where(low, cr, c) * c_ref[2:3, :]
    sin_b_ref[...] = jnp.where(low, sr, s) * c_ref[3:4, :]


def _rope_tables(pos_col):
    m = pos_col.shape[0]
    tm = 2048
    inv_a = ROPE_BASE ** (-jnp.arange(0, RET_HEAD_DIM, 2, dtype=F32) / RET_HEAD_DIM)
    inv_b = ROPE_BASE ** (-jnp.arange(0, MLA_ROPE_DIM, 2, dtype=F32) / MLA_ROPE_DIM)
    z32 = jnp.zeros((32,), F32)
    o32 = jnp.ones((32,), F32)
    o64 = jnp.ones((64,), F32)
    consts = jnp.stack([
        jnp.concatenate([inv_a, inv_b, z32]),
        jnp.concatenate([-o64, o64]),
        jnp.concatenate([o32, z32, o32, z32]),
        jnp.concatenate([-o32, z32, o32, z32]),
    ] + [jnp.zeros((LANES,), F32)] * 4)
    tab = jax.ShapeDtypeStruct((m, LANES), F32)
    return pl.pallas_call(
        _rope_tables_kernel,
        grid=(m // tm,),
        in_specs=[pl.BlockSpec((tm, 1), lambda i: (i, 0)),
                  pl.BlockSpec((SUBLANES, LANES), lambda i: (0, 0))],
        out_specs=[pl.BlockSpec((tm, LANES), lambda i: (i, 0))] * 4,
        out_shape=[tab] * 4,
        compiler_params=_params(("arbitrary",)),
        name="rope_tables",
    )(pos_col, consts)


def _norm_matmul_kernel(x_ref, g_ref, w_ref, o_ref, h_ref, *, rows):
    @pl.when(pl.program_id(1) == 0)
    def _():
        g = g_ref[...]
        for r in range(0, x_ref.shape[0], rows):
            x = x_ref[r:r + rows, :]
            ms = jnp.mean(x * x, axis=-1, keepdims=True)
            h_ref[r:r + rows, :] = (x * lax.rsqrt(ms + EPS) * g).astype(BF16)

    o_ref[...] = jnp.dot(h_ref[...], w_ref[...], preferred_element_type=F32).astype(o_ref.dtype)


def _norm_matmul(x, g, w, *, tm, tn):
    m, d = x.shape
    n = w.shape[1]
    return pl.pallas_call(
        functools.partial(_norm_matmul_kernel, rows=256),
        grid=(m // tm, n // tn),
        in_specs=[pl.BlockSpec((tm, d), lambda i, j: (i, 0)),
                  pl.BlockSpec((1, d), lambda i, j: (0, 0)),
                  pl.BlockSpec((d, tn), lambda i, j: (0, j))],
        out_specs=pl.BlockSpec((tm, tn), lambda i, j: (i, j)),
        out_shape=jax.ShapeDtypeStruct((m, n), BF16),
        scratch_shapes=[pltpu.VMEM((tm, d), BF16)],
        compiler_params=_params(("parallel", "arbitrary")),
        name="norm_matmul",
    )(x, g, w)


def _rope_half(x, cos, sin):
    return x * cos + pltpu.roll(x, 64, 1) * sin


def _mla_prep_kernel(cq_ref, ckv_ref, kr_ref, gqa_ref, gkva_ref, wq_ref, wkv_ref, gq_ref, gk_ref,
                     cos_ref, sin_ref, q_ref, k_ref, v_ref):
    def latent(ref, g_ref_):
        c = ref[...].astype(F32)
        ms = jnp.mean(c * c, axis=-1, keepdims=True)
        return (c * lax.rsqrt(ms + EPS) * g_ref_[...]).astype(BF16)

    qf = jnp.dot(latent(cq_ref, gqa_ref), wq_ref[...], preferred_element_type=F32)
    kvf = jnp.dot(latent(ckv_ref, gkva_ref), wkv_ref[...], preferred_element_type=F32)
    cos = cos_ref[...]
    sin = sin_ref[...]
    kr = kr_ref[...].astype(F32)
    kr_ss = jnp.sum(kr * kr, axis=-1, keepdims=True)
    gq_n, gq_r = gq_ref[:, :LANES], gq_ref[:, LANES:]
    gk_n, gk_r = gk_ref[:, :LANES], gk_ref[:, LANES:]
    scale = MLA_QK_DIM ** -0.5
    inv_d = 1.0 / MLA_QK_DIM
    for h in range(N_MLA_HEADS):
        c0 = h * MLA_HEAD_PAD
        qn = qf[:, c0:c0 + LANES]
        qr = qf[:, c0 + LANES:c0 + MLA_HEAD_PAD]
        ss = jnp.sum(qn * qn, axis=-1, keepdims=True) + jnp.sum(qr * qr, axis=-1, keepdims=True)
        rq = lax.rsqrt(ss * inv_d + EPS)
        q_ref[:, c0:c0 + LANES] = (qn * rq * gq_n * scale).astype(BF16)
        q_ref[:, c0 + LANES:c0 + MLA_HEAD_PAD] = (_rope_half(qr * rq * gq_r, cos, sin) * scale).astype(BF16)
        kn = kvf[:, c0:c0 + LANES]
        ssk = jnp.sum(kn * kn, axis=-1, keepdims=True) + kr_ss
        rk = lax.rsqrt(ssk * inv_d + EPS)
        k_ref[:, c0:c0 + LANES] = (kn * rk * gk_n).astype(BF16)
        k_ref[:, c0 + LANES:c0 + MLA_HEAD_PAD] = _rope_half(kr * rk * gk_r, cos, sin).astype(BF16)
        v_ref[:, h * MLA_V_DIM:(h + 1) * MLA_V_DIM] = kvf[:, c0 + LANES:c0 + MLA_HEAD_PAD].astype(BF16)


def _mla_prep(proj, gqa, gkva, wq, wkv, gq, gk, cos_b, sin_b, *, tm):
    m = proj.shape[0]
    hp = N_MLA_HEADS * MLA_HEAD_PAD
    const = lambda i: (0, 0)
    return pl.pallas_call(
        _mla_prep_kernel,
        grid=(m // tm,),
        in_specs=[pl.BlockSpec((tm, MLA_Q_RANK), lambda i: (i, COL_CQ // MLA_Q_RANK)),
                  pl.BlockSpec((tm, MLA_KV_RANK), lambda i: (i, COL_CKV // MLA_KV_RANK)),
                  pl.BlockSpec((tm, LANES), lambda i: (i, COL_KROPE // LANES)),
                  pl.BlockSpec((1, MLA_Q_RANK), const),
                  pl.BlockSpec((1, MLA_KV_RANK), const),
                  pl.BlockSpec((MLA_Q_RANK, hp), const),
                  pl.BlockSpec((MLA_KV_RANK, hp), const),
                  pl.BlockSpec((1, MLA_HEAD_PAD), const),
                  pl.BlockSpec((1, MLA_HEAD_PAD), const),
                  pl.BlockSpec((tm, LANES), lambda i: (i, 0)),
                  pl.BlockSpec((tm, LANES), lambda i: (i, 0))],
        out_specs=[pl.BlockSpec((tm, hp), lambda i: (i, 0)),
                   pl.BlockSpec((tm, hp), lambda i: (i, 0)),
                   pl.BlockSpec((tm, MLA_WIDTH), lambda i: (i, 0))],
        out_shape=[jax.ShapeDtypeStruct((m, hp), BF16),
                   jax.ShapeDtypeStruct((m, hp), BF16),
                   jax.ShapeDtypeStruct((m, MLA_WIDTH), BF16)],
        compiler_params=_params(("parallel",)),
        name="mla_prep",
    )(proj, proj, proj, gqa, gkva, wq, wkv, gq, gk, cos_b, sin_b)


def _attn_kernel(q_ref, k_ref, v_ref, o_ref, m_ref, l_ref, acc_ref):
    i = pl.program_id(2)
    q = q_ref[...]
    m_ref[...] = jnp.full(m_ref.shape, -jnp.inf, F32)
    l_ref[...] = jnp.zeros(l_ref.shape, F32)
    acc_ref[...] = jnp.zeros(acc_ref.shape, F32)

    def step(j, masked):
        start = pl.multiple_of(j * ATT_TK, ATT_TK)
        k = k_ref[pl.ds(start, ATT_TK), :]
        v = v_ref[pl.ds(start, ATT_TK), :]
        s = lax.dot_general(q, k, (((1,), (1,)), ((), ())), preferred_element_type=F32)
        if masked:
            row = lax.broadcasted_iota(jnp.int32, s.shape, 0)
            col = lax.broadcasted_iota(jnp.int32, s.shape, 1)
            s = jnp.where(col <= row, s, NEG_BIG)
        m_prev = m_ref[...]
        m_new = jnp.maximum(m_prev, jnp.max(s, axis=-1, keepdims=True))
        alpha = jnp.exp(m_prev - m_new)
        p = jnp.exp(s - m_new)
        l_ref[...] = alpha * l_ref[...] + jnp.sum(p, axis=-1, keepdims=True)
        acc_ref[...] = alpha * acc_ref[...] + jnp.dot(p.astype(BF16), v, preferred_element_type=F32)
        m_ref[...] = m_new

    def body(j, carry):
        step(j, False)
        return carry

    lax.fori_loop(0, i, body, 0)
    step(i, True)
    o_ref[...] = (acc_ref[...] / l_ref[...]).astype(o_ref.dtype)


def _attention(q, k, v, *, batch, seq):
    m = q.shape[0]
    nq = seq // ATT_TQ
    return pl.pallas_call(
        _attn_kernel,
        grid=(batch, N_MLA_HEADS, nq),
        in_specs=[pl.BlockSpec((ATT_TQ, MLA_HEAD_PAD), lambda b, h, i: (b * nq + i, h)),
                  pl.BlockSpec((seq, MLA_HEAD_PAD), lambda b, h, i: (b, h)),
                  pl.BlockSpec((seq, MLA_V_DIM), lambda b, h, i: (b, h))],
        out_specs=pl.BlockSpec((ATT_TQ, MLA_V_DIM), lambda b, h, i: (b * nq + i, h)),
        out_shape=jax.ShapeDtypeStruct((m, MLA_WIDTH), BF16),
        scratch_shapes=[pltpu.VMEM((ATT_TQ, 1), F32),
                        pltpu.VMEM((ATT_TQ, 1), F32),
                        pltpu.VMEM((ATT_TQ, MLA_V_DIM), F32)],
        compiler_params=_params(("parallel", "parallel", "arbitrary")),
        name="mla_attention",
    )(q, k, v)


def _retention_kernel(q_ref, k_ref, v_ref, cos_ref, sin_ref, g_ref, o_ref, state_ref, dmask_ref):
    c = RET_CHUNK

    @pl.when(pl.program_id(1) == 0)
    def _():
        state_ref[...] = jnp.zeros(state_ref.shape, F32)
        row = lax.broadcasted_iota(jnp.int32, (c, c), 0)
        col = lax.broadcasted_iota(jnp.int32, (c, c), 1)
        rel = (row - col).astype(F32)
        for h in range(N_RET_HEADS):
            dmask_ref[h] = jnp.where(rel >= 0.0, jnp.exp(RET_LOG_G[h] * jnp.maximum(rel, 0.0)), 0.0)

    cos = cos_ref[...]
    sin = sin_ref[...]
    idx = lax.broadcasted_iota(jnp.int32, (c, 1), 0).astype(F32)
    k_scale = RET_HEAD_DIM ** -0.5
    for h in range(N_RET_HEADS):
        sl = slice(h * RET_HEAD_DIM, (h + 1) * RET_HEAD_DIM)
        q = _rope_half(q_ref[:, sl].astype(F32), cos, sin)
        k = _rope_half(k_ref[:, sl].astype(F32), cos, sin) * k_scale
        v = v_ref[:, sl]
        qb = q.astype(BF16)
        scores = lax.dot_general(qb, k.astype(BF16), (((1,), (1,)), ((), ())),
                                 preferred_element_type=F32) * dmask_ref[h]
        inner = jnp.dot(scores.astype(BF16), v, preferred_element_type=F32)
        st = state_ref[h]
        from_start = jnp.exp(RET_LOG_G[h] * (idx + 1.0))
        cross = jnp.dot(qb, st.astype(BF16), preferred_element_type=F32) * from_start
        to_end = jnp.exp(RET_LOG_G[h] * (c - 1.0 - idx))
        kd_t = (k * to_end).T.astype(BF16)
        state_ref[h] = math.exp(RET_LOG_G[h] * c) * st + jnp.dot(kd_t, v, preferred_element_type=F32)
        o = inner + cross
        ms = jnp.mean(o * o, axis=-1, keepdims=True)
        o_ref[:, sl] = (o * lax.rsqrt(ms + EPS) * g_ref[:, sl]).astype(o_ref.dtype)


def _retention(proj, cos_a, sin_a, g, *, batch, seq):
    m = proj.shape[0]
    nc = seq // RET_CHUNK
    row = lambda b, c: b * nc + c
    return pl.pallas_call(
        _retention_kernel,
        grid=(batch, nc),
        in_specs=[pl.BlockSpec((RET_CHUNK, RET_WIDTH), lambda b, c: (row(b, c), COL_RQ // RET_WIDTH)),
                  pl.BlockSpec((RET_CHUNK, RET_WIDTH), lambda b, c: (row(b, c), COL_RK // RET_WIDTH)),
                  pl.BlockSpec((RET_CHUNK, RET_WIDTH), lambda b, c: (row(b, c), COL_RV // RET_WIDTH)),
                  pl.BlockSpec((RET_CHUNK, LANES), lambda b, c: (row(b, c), 0)),
                  pl.BlockSpec((RET_CHUNK, LANES), lambda b, c: (row(b, c), 0)),
                  pl.BlockSpec((1, RET_WIDTH), lambda b, c: (0, 0))],
        out_specs=pl.BlockSpec((RET_CHUNK, RET_WIDTH), lambda b, c: (row(b, c), 0)),
        out_shape=jax.ShapeDtypeStruct((m, RET_WIDTH), BF16),
        scratch_shapes=[pltpu.VMEM((N_RET_HEADS, RET_HEAD_DIM, RET_HEAD_DIM), F32),
                        pltpu.VMEM((N_RET_HEADS, RET_CHUNK, RET_CHUNK), F32)],
        compiler_params=_params(("parallel", "arbitrary")),
        name="retention",
    )(proj, proj, proj, cos_a, sin_a, g)


def _silu(x):
    return x * jax.nn.sigmoid(x)


def _mix_out_kernel(ret_ref, att_ref, gate_ref, w_ref, x_ref, o_ref, mix_ref, *, rows):
    @pl.when(pl.program_id(1) == 0)
    def _():
        for r in range(0, ret_ref.shape[0], rows):
            sg = _silu(gate_ref[r:r + rows, :].astype(F32))
            mix_ref[r:r + rows, :RET_WIDTH] = (ret_ref[r:r + rows, :].astype(F32) * sg[:, :RET_WIDTH]).astype(BF16)
            mix_ref[r:r + rows, RET_WIDTH:] = (att_ref[r:r + rows, :].astype(F32) * sg[:, RET_WIDTH:]).astype(BF16)

    o_ref[...] = x_ref[...] + jnp.dot(mix_ref[...], w_ref[...], preferred_element_type=F32)


def _mix_out(ret, att, proj, w, x, *, tm, tn):
    m, d = x.shape
    return pl.pallas_call(
        functools.partial(_mix_out_kernel, rows=256),
        grid=(m // tm, d // tn),
        in_specs=[pl.BlockSpec((tm, RET_WIDTH), lambda i, j: (i, 0)),
                  pl.BlockSpec((tm, MLA_WIDTH), lambda i, j: (i, 0)),
                  pl.BlockSpec((tm, MIX_WIDTH), lambda i, j: (i, COL_GATE // MIX_WIDTH)),
                  pl.BlockSpec((MIX_WIDTH, tn), lambda i, j: (0, j)),
                  pl.BlockSpec((tm, tn), lambda i, j: (i, j))],
        out_specs=pl.BlockSpec((tm, tn), lambda i, j: (i, j)),
        out_shape=jax.ShapeDtypeStruct((m, d), F32),
        scratch_shapes=[pltpu.VMEM((tm, MIX_WIDTH), BF16)],
        compiler_params=_params(("parallel", "arbitrary")),
        name="mix_out",
    )(ret, att, proj, w, x)


def _glu_conv_kernel(a_ref, b_ref, ap_ref, bp_ref, w_ref, bias_ref, y_ref, u_ref, *, tiles_per_seq):
    tm, tc = a_ref.shape
    first = (pl.program_id(0) % tiles_per_seq) == 0
    halo = ap_ref[...].astype(F32) * jax.nn.sigmoid(bp_ref[...].astype(F32))
    u_ref[0:CONV_HALO, :] = jnp.where(first, 0.0, halo)
    u_ref[CONV_HALO:CONV_HALO + tm, :] = a_ref[...].astype(F32) * jax.nn.sigmoid(b_ref[...].astype(F32))
    u_ref[CONV_HALO + tm:, :] = jnp.zeros((SUBLANES, tc), F32)
    bias = bias_ref[...]
    first_off = CONV_HALO - (CONV_KERNEL - 1)

    def chunk(r, carry):
        base = pl.multiple_of(r * CONV_ROWS, CONV_ROWS)
        for c0 in range(0, tc, 2 * LANES):
            cs = slice(c0, c0 + 2 * LANES)
            out = jnp.broadcast_to(bias[:, cs], (CONV_ROWS, 2 * LANES))
            for s in range(SUBLANES):
                part = None
                for tap in range(CONV_KERNEL):
                    off = first_off + tap
                    if off % SUBLANES != s:
                        continue
                    win = u_ref[pl.ds(base + (off - s), CONV_ROWS + SUBLANES), cs]
                    term = win * w_ref[tap:tap + 1, cs]
                    part = term if part is None else part + term
                if part is not None:
                    out = out + part[s:s + CONV_ROWS, :]
            y_ref[pl.ds(base, CONV_ROWS), cs] = out
        return carry

    lax.fori_loop(0, tm // CONV_ROWS, chunk, 0)


def _glu_conv(proj, w, bias, *, seq, tm, tc):
    m = proj.shape[0]
    n_cb = CONV_WIDTH // tc
    halo_blocks = tm // CONV_HALO
    prev = lambda i: jnp.maximum(i * halo_blocks - 1, 0)
    return pl.pallas_call(
        functools.partial(_glu_conv_kernel, tiles_per_seq=seq // tm),
        grid=(m // tm, n_cb),
        in_specs=[pl.BlockSpec((tm, tc), lambda i, j: (i, j)),
                  pl.BlockSpec((tm, tc), lambda i, j: (i, n_cb + j)),
                  pl.BlockSpec((CONV_HALO, tc), lambda i, j: (prev(i), j)),
                  pl.BlockSpec((CONV_HALO, tc), lambda i, j: (prev(i), n_cb + j)),
                  pl.BlockSpec((CONV_HALO, tc), lambda i, j: (0, j)),
                  pl.BlockSpec((1, tc), lambda i, j: (0, j))],
        out_specs=pl.BlockSpec((tm, tc), lambda i, j: (i, j)),
        out_shape=jax.ShapeDtypeStruct((m, CONV_WIDTH), F32),
        scratch_shapes=[pltpu.VMEM((tm + CONV_HALO + SUBLANES, tc), F32)],
        compiler_params=_params(("parallel", "parallel")),
        name="glu_conv",
    )(proj, proj, proj, proj, w, bias)


def _ln_out_kernel(y_ref, gate_ref, lg_ref, lb_ref, w_ref, x_ref, o_ref, mix_ref, *, rows):
    @pl.when(pl.program_id(1) == 0)
    def _():
        lg = lg_ref[...]
        lb = lb_ref[...]
        for r in range(0, y_ref.shape[0], rows):
            y = y_ref[r:r + rows, :]
            mu = jnp.mean(y, axis=-1, keepdims=True)
            yc = y - mu
            var = jnp.mean(yc * yc, axis=-1, keepdims=True)
            u = _silu(yc * lax.rsqrt(var + EPS) * lg + lb)
            mix_ref[r:r + rows, :] = (u * _silu(gate_ref[r:r + rows, :].astype(F32))).astype(BF16)

    o_ref[...] = x_ref[...] + jnp.dot(mix_ref[...], w_ref[...], preferred_element_type=F32)


def _ln_out(y, proj, lg, lb, w, x, *, tm, tn):
    m, d = x.shape
    return pl.pallas_call(
        functools.partial(_ln_out_kernel, rows=256),
        grid=(m // tm, d // tn),
        in_specs=[pl.BlockSpec((tm, CONV_WIDTH), lambda i, j: (i, 0)),
                  pl.BlockSpec((tm, CONV_WIDTH), lambda i, j: (i, 2)),
                  pl.BlockSpec((1, CONV_WIDTH), lambda i, j: (0, 0)),
                  pl.BlockSpec((1, CONV_WIDTH), lambda i, j: (0, 0)),
                  pl.BlockSpec((CONV_WIDTH, tn), lambda i, j: (0, j)),
                  pl.BlockSpec((tm, tn), lambda i, j: (i, j))],
        out_specs=pl.BlockSpec((tm, tn), lambda i, j: (i, j)),
        out_shape=jax.ShapeDtypeStruct((m, d), F32),
        scratch_shapes=[pltpu.VMEM((tm, CONV_WIDTH), BF16)],
        compiler_params=_params(("parallel", "arbitrary")),
        name="ln_out",
    )(y, proj, lg, lb, w, x)


def _rope_cols_padded(w):
    z = jnp.zeros(w.shape[:-1] + (32,), w.dtype)
    return jnp.concatenate([w[..., :32], z, w[..., 32:], z], axis=-1)


def _head_cols_padded(w):
    lead = w.shape[:-1]
    wh = w.reshape(lead + (N_MLA_HEADS, MLA_QK_DIM))
    out = jnp.concatenate([wh[..., :MLA_NOPE_DIM], _rope_cols_padded(wh[..., MLA_NOPE_DIM:])], axis=-1)
    return out.reshape(lead + (N_MLA_HEADS * MLA_HEAD_PAD,))


def _layer_retention_mla(x2, cos_a, sin_a, cos_b, sin_b, norm_g, w_in, q_a_norm_g, w_q_b, kv_a_norm_g,
                         w_kv_b, q_norm_g, k_norm_g, ret_norm_g, w_out, *, batch, seq):
    r, q, kv = RET_WIDTH, MLA_Q_RANK, MLA_KV_RANK
    w_in_r = jnp.concatenate([
        w_in[:, 3 * r + q + kv + MLA_ROPE_DIM:],
        w_in[:, :3 * r + q + kv],
        _rope_cols_padded(w_in[:, 3 * r + q + kv:3 * r + q + kv + MLA_ROPE_DIM]),
    ], axis=1).astype(BF16)
    proj = _norm_matmul(x2, norm_g[None, :], w_in_r, tm=1024, tn=896)
    gq = jnp.concatenate([q_norm_g[:MLA_NOPE_DIM], _rope_cols_padded(q_norm_g[MLA_NOPE_DIM:])])[None, :]
    gk = jnp.concatenate([k_norm_g[:MLA_NOPE_DIM], _rope_cols_padded(k_norm_g[MLA_NOPE_DIM:])])[None, :]
    qm, km, vm = _mla_prep(proj, q_a_norm_g[None, :], kv_a_norm_g[None, :],
                           _head_cols_padded(w_q_b).astype(BF16), w_kv_b.astype(BF16),
                           gq, gk, cos_b, sin_b, tm=256)
    att = _attention(qm, km, vm, batch=batch, seq=seq)
    ret = _retention(proj, cos_a, sin_a, ret_norm_g[None, :], batch=batch, seq=seq)
    return _mix_out(ret, att, proj, w_out.astype(BF16), x2, tm=512, tn=1024)


def _layer_conformer_conv(x2, norm_g, w_in, conv_w, conv_b, ln_g, ln_b, w_out, *, seq):
    proj = _norm_matmul(x2, norm_g[None, :], w_in.astype(BF16), tm=1024, tn=1024)
    w_pad = jnp.concatenate([conv_w, jnp.zeros((CONV_HALO - CONV_KERNEL, CONV_WIDTH), conv_w.dtype)], axis=0)
    y = _glu_conv(proj, w_pad, conv_b[None, :], seq=seq, tm=512, tc=512)
    return _ln_out(y, proj, ln_g[None, :], ln_b[None, :], w_out.astype(BF16), x2, tm=512, tn=1024)


def kernel(x, positions, a_norm_g, a_w_in, a_q_a_norm_g, a_w_q_b, a_kv_a_norm_g, a_w_kv_b, a_q_norm_g, a_k_norm_g, a_ret_norm_g, a_w_out, c_norm_g, c_w_in, c_conv_w, c_conv_b, c_ln_g, c_ln_b, c_w_out):
    batch, seq, d = x.shape
    depth = a_norm_g.shape[0] + c_norm_g.shape[0]
    x2 = x.reshape(batch * seq, d)
    cos_a, sin_a, cos_b, sin_b = _rope_tables(positions.reshape(batch * seq, 1))
    for layer in range(depth):
        i = layer // 2
        if layer % 2 == 0:
            x2 = _layer_retention_mla(x2, cos_a, sin_a, cos_b, sin_b, a_norm_g[i], a_w_in[i], a_q_a_norm_g[i],
                                      a_w_q_b[i], a_kv_a_norm_g[i], a_w_kv_b[i], a_q_norm_g[i], a_k_norm_g[i],
                                      a_ret_norm_g[i], a_w_out[i], batch=batch, seq=seq)
        else:
            x2 = _layer_conformer_conv(x2, c_norm_g[i], c_w_in[i], c_conv_w[i], c_conv_b[i], c_ln_g[i],
                                       c_ln_b[i], c_w_out[i], seq=seq)
    return x2.reshape(batch, seq, d)
```
